```python
import jax, jax.numpy as jnp
from jax import lax
import numpy as np

D_MODEL = 2048
BATCH = 32
SEQ = 256
DEPTH = 4
DEC_BATCH = 8
DEC_SEQ = 2048
PAST_LEN = 256

GRID_W = 64
N_MIXERS = 4
D_FF = 5632
N_MOD = 9
RMS_EPS = 1e-6
FFN_RES_WEIGHT = 0.5
MIXER_RES_WEIGHT = 1.0

POOL_WINDOWS = (2, 4, 8, 16)
POOL_GROUPS = len(POOL_WINDOWS)
POOL_GW = D_MODEL // POOL_GROUPS

MLSTM_HEADS = 8
MLSTM_DV = D_MODEL // MLSTM_HEADS
MLSTM_DK = MLSTM_DV // 2
MLSTM_QK = MLSTM_HEADS * MLSTM_DK
MLSTM_V = MLSTM_HEADS * MLSTM_DV
MLSTM_IN = 2 * MLSTM_QK + 2 * MLSTM_V + 4 * MLSTM_HEADS
MLSTM_CHUNK = 64
F_BIAS_OFFSET = 3.0

FOURIER_GROUPS = 4
FOURIER_GW = D_MODEL // FOURIER_GROUPS

NA_HEADS = 16
NA_HD = D_MODEL // NA_HEADS
NA_WIN_ROWS = 8
NA_WIN_COLS = 16
NA_Q_BLOCK = 128

N_POOL_LAYERS = (DEPTH + N_MIXERS - 1) // N_MIXERS
N_MLSTM_LAYERS = (DEPTH + N_MIXERS - 2) // N_MIXERS
N_FOURIER_LAYERS = (DEPTH + N_MIXERS - 3) // N_MIXERS
N_NA_LAYERS = (DEPTH + N_MIXERS - 4) // N_MIXERS

kernel_name = 'hybrid_diffusion_interleaved_step'


def _rms_norm(x, g):
    xf = x.astype(jnp.float32)
    y = xf * lax.rsqrt(jnp.mean(xf * xf, axis=-1, keepdims=True) + RMS_EPS)
    return (y * g.astype(jnp.float32)).astype(x.dtype)


def _modulation(cond, w, b):
    m = jax.nn.silu(cond) @ w + b
    return m.reshape(cond.shape[0], N_MOD, D_MODEL)


def _pre(x, g, shift, scale):
    return _rms_norm(x, g) * (1 + scale[:, None]) + shift[:, None]


def _residual(x, y, g, gate, weight):
    return x + weight * gate[:, None] * _rms_norm(y, g)


def _swiglu(h, wg, wu, wd):
    return (jax.nn.silu(h @ wg) * (h @ wu)) @ wd


def _ffn_sublayer(x, mod, k0, g_pre, g_post, wg, wu, wd):
    h = _pre(x, g_pre, mod[:, k0], mod[:, k0 + 1])
    return _residual(x, _swiglu(h, wg, wu, wd), g_post, mod[:, k0 + 2], FFN_RES_WEIGHT)


def _pool_mix(h, w, scale):
    b, t, _ = h.shape
    hf = h.astype(jnp.float32).reshape(b, t, POOL_GROUPS, POOL_GW)
    cs = jnp.concatenate([jnp.zeros((b, 1, POOL_GROUPS, POOL_GW), jnp.float32),
                          jnp.cumsum(hf, axis=1)], axis=1)
    pos = jnp.arange(t)
    pooled = []
    for gi, win in enumerate(POOL_WINDOWS):
        lo = jnp.clip(pos - win // 2, 0, t)
        hi = jnp.clip(pos - win // 2 + win, 0, t)
        cnt = (hi - lo).astype(jnp.float32)[None, :, None]
        pooled.append((cs[:, hi, gi] - cs[:, lo, gi]) / cnt)
    pooled = jnp.stack(pooled, axis=2)
    y = jnp.einsum('btgc,gcd->btgd', pooled - hf, w.astype(jnp.float32)).reshape(b, t, D_MODEL)
    return (y * scale.astype(jnp.float32)).astype(h.dtype)


def _fourier_mix(h, w_out):
    b, t, _ = h.shape
    hg = h.astype(jnp.float32).reshape(b, t, FOURIER_GROUPS, FOURIER_GW)
    f = jnp.fft.fft2(hg, axes=(1, 3), norm='ortho').real
    return f.reshape(b, t, D_MODEL).astype(h.dtype) @ w_out


def _mlstm_chunkwise(q, k, v, log_i, log_f, init):
    b, nh, t, _ = q.shape
    L = MLSTM_CHUNK
    nc = t // L

    def to_chunks(a):
        return jnp.moveaxis(a.reshape(b, nh, nc, L, *a.shape[3:]), 2, 0)

    xs = (to_chunks(q), to_chunks(k), to_chunks(v), to_chunks(log_i), to_chunks(log_f))
    causal = jnp.tril(jnp.ones((L, L), dtype=bool))

    def step(carry, inp):
        C, n, m = carry
        qc, kc, vc, ic, fc = inp
        bcum = jnp.cumsum(fc, axis=-1)
        dmat = bcum[..., :, None] - bcum[..., None, :] + ic[..., None, :]
        dmat = jnp.where(causal, dmat, -jnp.inf)
        inter = bcum + m[..., None]
        mt = jnp.maximum(inter, jnp.max(dmat, axis=-1))
        w_inter = jnp.exp(inter - mt)
        s = jnp.einsum('bhtd,bhsd->bhts', qc, kc) * jnp.exp(dmat - mt[..., None])
        num = w_inter[..., None] * jnp.einsum('bhtd,bhde->bhte', qc, C) + jnp.einsum('bhts,bhse->bhte', s, vc)
        den = w_inter * jnp.einsum('bhtd,bhd->bht', qc, n) + jnp.sum(s, axis=-1)
        h = num / jnp.maximum(jnp.abs(den), jnp.exp(-mt))[..., None]
        m_new = mt[..., -1]
        decay = jnp.exp(bcum[..., -1] + m - m_new)
        ws = jnp.exp(bcum[..., -1:] - bcum + ic - m_new[..., None])
        C_new = decay[..., None, None] * C + jnp.einsum('bhs,bhsd,bhse->bhde', ws, kc, vc)
        n_new = decay[..., None] * n + jnp.einsum('bhs,bhsd->bhd', ws, kc)
        return (C_new, n_new, m_new), h

    final, hs = lax.scan(step, init, xs)
    h = jnp.moveaxis(hs, 0, 2).reshape(b, nh, t, v.shape[-1])
    return h, final


def _mlstm_mix(h, w_in, b_gates, w_out, init_f, init_b):
    b, t, _ = h.shape
    p = (h @ w_in).astype(jnp.float32)
    q, k, v, o, g = jnp.split(p, [MLSTM_QK, 2 * MLSTM_QK, 2 * MLSTM_QK + MLSTM_V,
                                  2 * MLSTM_QK + 2 * MLSTM_V], axis=-1)

    def heads(a, dh):
        return a.reshape(b, t, MLSTM_HEADS, dh).transpose(0, 2, 1, 3)

    q = heads(q, MLSTM_DK) * MLSTM_DK ** -0.5
    k = heads(k, MLSTM_DK)
    v = heads(v, MLSTM_DV)
    g = (g + b_gates.astype(jnp.float32)).reshape(b, t, 4, MLSTM_HEADS).transpose(2, 0, 3, 1)
    h_f, st_f = _mlstm_chunkwise(q, k, v, g[0], jax.nn.log_sigmoid(g[1]), init_f)

    def rev(a):
        return jnp.flip(a, axis=2)

    h_b, st_b = _mlstm_chunkwise(rev(q), rev(k), rev(v), rev(g[2]), rev(jax.nn.log_sigmoid(g[3])), init_b)
    hs = (h_f + rev(h_b)).transpose(0, 2, 1, 3).reshape(b, t, MLSTM_V)
    y = (jax.nn.sigmoid(o) * hs).astype(h.dtype) @ w_out
    return y, st_f, st_b


def _na_project(h, w_qkv):
    b, t, _ = h.shape
    qkv = (h @ w_qkv).reshape(b, t, 3, NA_HEADS, NA_HD)
    return qkv[:, :, 0], qkv[:, :, 1], qkv[:, :, 2]


def _ctx_self_attention(q, k, v):
    b, s, nh, dh = q.shape
    nb = s // NA_Q_BLOCK
    qb = jnp.moveaxis(q.reshape(b, nb, NA_Q_BLOCK, nh, dh), 1, 0)

    def blk(qi):
        sc = jnp.einsum('bqhd,bkhd->bhqk', qi, k).astype(jnp.float32) * dh ** -0.5
        pr = jax.nn.softmax(sc, axis=-1).astype(v.dtype)
        return jnp.einsum('bhqk,bkhd->bqhd', pr, v)

    o = lax.map(blk, qb)
    return jnp.moveaxis(o, 0, 1).reshape(b, s, nh * dh)


def _neighbourhood_attention(q, k, v, k_ctx, v_ctx, rel_bias):
    b, t, nh, dh = q.shape
    rows = t // GRID_W
    wr = min(NA_WIN_ROWS, rows)
    scale = dh ** -0.5
    kg = k.reshape(b, rows, GRID_W, nh, dh)
    vg = v.reshape(b, rows, GRID_W, nh, dh)
    qg = q.reshape(b, rows, GRID_W, nh, dh)
    col = jnp.arange(GRID_W)
    col_start = jnp.clip(col - NA_WIN_COLS // 2, 0, GRID_W - NA_WIN_COLS)
    col_mask = (col[None, :] >= col_start[:, None]) & (col[None, :] < col_start[:, None] + NA_WIN_COLS)
    col_idx = jnp.clip(col[None, :] - col[:, None] + NA_WIN_COLS - 1, 0, 2 * NA_WIN_COLS - 2)
    bias_cols = rel_bias[:, :, col_idx].astype(jnp.float32)

    def row_fn(args):
        r, qr = args
        sr = jnp.clip(r - wr // 2, 0, rows - wr)
        kb = lax.dynamic_slice_in_dim(kg, sr, wr, axis=1)
        vb = lax.dynamic_slice_in_dim(vg, sr, wr, axis=1)
        row_off = sr + jnp.arange(wr) - r + NA_WIN_ROWS - 1
        bias = jnp.transpose(bias_cols[:, row_off], (0, 2, 1, 3))
        s_loc = jnp.einsum('bqhd,bawhd->bhqaw', qr, kb).astype(jnp.float32) * scale + bias[None]
        s_loc = jnp.where(col_mask[:, None, :], s_loc, -jnp.inf)
        s_ctx = jnp.einsum('bqhd,bphd->bhqp', qr, k_ctx).astype(jnp.float32) * scale
        sc = jnp.concatenate([s_loc.reshape(b, nh, GRID_W, wr * GRID_W), s_ctx], axis=-1)
        pr = jax.nn.softmax(sc, axis=-1).astype(v.dtype)
        p_loc = pr[..., :wr * GRID_W].reshape(b, nh, GRID_W, wr, GRID_W)
        p_ctx = pr[..., wr * GRID_W:]
        return (jnp.einsum('bhqaw,bawhd->bqhd', p_loc, vb)
                + jnp.einsum('bhqp,bphd->bqhd', p_ctx, v_ctx))

    o = lax.map(row_fn, (jnp.arange(rows), jnp.moveaxis(qg, 1, 0)))
    return jnp.moveaxis(o, 0, 1).reshape(b, t, nh * dh)


def setup_inputs(seed: int = 0) -> dict:
    key = jax.random.key(seed)
    ks = jax.random.split(key, 24)
    f32 = jnp.float32

    def nrm(k, shape, s):
        return jax.random.normal(k, shape, f32) * s

    hb = MLSTM_HEADS
    gate_offset = jnp.concatenate([jnp.zeros((hb,), f32), jnp.full((hb,), F_BIAS_OFFSET, f32),
                                   jnp.zeros((hb,), f32), jnp.full((hb,), F_BIAS_OFFSET, f32)])
    return {
        'x_prompt': nrm(ks[0], (BATCH, SEQ, D_MODEL), 1.0),
        'x_sample': nrm(ks[1], (DEC_BATCH, DEC_SEQ, D_MODEL), 1.0),
        'cache_na_k': nrm(ks[2], (DEC_BATCH, N_NA_LAYERS, PAST_LEN, NA_HEADS, NA_HD), 1.0),
        'cache_na_v': nrm(ks[3], (DEC_BATCH, N_NA_LAYERS, PAST_LEN, NA_HEADS, NA_HD), 1.0),
        'state_mlstm_C': nrm(ks[4], (DEC_BATCH, N_MLSTM_LAYERS, 2, MLSTM_HEADS, MLSTM_DK, MLSTM_DV), 0.1),
        'state_mlstm_n': nrm(ks[5], (DEC_BATCH, N_MLSTM_LAYERS, 2, MLSTM_HEADS, MLSTM_DK), 0.1),
        'state_mlstm_m': nrm(ks[6], (DEC_BATCH, N_MLSTM_LAYERS, 2, MLSTM_HEADS), 1.0),
        'c': nrm(ks[7], (DEC_BATCH, D_MODEL), 1.0),
        'c_ctx': nrm(ks[8], (D_MODEL,), 1.0),
        'w_mod': nrm(ks[9], (DEPTH, D_MODEL, N_MOD * D_MODEL), 0.5 * D_MODEL ** -0.5),
        'b_mod': nrm(ks[10], (DEPTH, N_MOD * D_MODEL), 0.02),
        'norm_g': 1.0 + nrm(ks[11], (DEPTH, 6, D_MODEL), 0.05),
        'ffn_w_gate': nrm(ks[12], (DEPTH, 2, D_MODEL, D_FF), D_MODEL ** -0.5),
        'ffn_w_up': nrm(ks[13], (DEPTH, 2, D_MODEL, D_FF), D_MODEL ** -0.5),
        'ffn_w_down': nrm(ks[14], (DEPTH, 2, D_FF, D_MODEL), D_FF ** -0.5),
        'pool_w': nrm(ks[15], (N_POOL_LAYERS, POOL_GROUPS, POOL_GW, POOL_GW), POOL_GW ** -0.5),
        'pool_scale': 1.0 + nrm(ks[16], (N_POOL_LAYERS, D_MODEL), 0.1),
        'mlstm_w_in': nrm(ks[17], (N_MLSTM_LAYERS, D_MODEL, MLSTM_IN), D_MODEL ** -0.5),
        'mlstm_b_gates': gate_offset + nrm(ks[18], (N_MLSTM_LAYERS, 4 * MLSTM_HEADS), 0.1),
        'mlstm_w_out': nrm(ks[19], (N_MLSTM_LAYERS, MLSTM_V, D_MODEL), MLSTM_V ** -0.5),
        'fourier_w_out': nrm(ks[20], (N_FOURIER_LAYERS, D_MODEL, D_MODEL), D_MODEL ** -0.5),
        'na_w_qkv': nrm(ks[21], (N_NA_LAYERS, D_MODEL, 3 * D_MODEL), D_MODEL ** -0.5),
        'na_rel_bias': nrm(ks[22], (N_NA_LAYERS, NA_HEADS, 2 * NA_WIN_ROWS - 1, 2 * NA_WIN_COLS - 1), 0.5),
        'na_w_out': nrm(ks[23], (N_NA_LAYERS, D_MODEL, D_MODEL), D_MODEL ** -0.5),
    }


def reference(x_prompt, x_sample, cache_na_k, cache_na_v, state_mlstm_C, state_mlstm_n, state_mlstm_m,
              c, c_ctx, w_mod, b_mod, norm_g, ffn_w_gate, ffn_w_up, ffn_w_down, pool_w, pool_scale,
              mlstm_w_in, mlstm_b_gates, mlstm_w_out, fourier_w_out, na_w_qkv, na_rel_bias, na_w_out):
    f32 = jnp.float32
    xc, xl = x_prompt, x_sample
    bc = x_prompt.shape[0]
    new_k, new_v, new_C, new_n, new_m = [], [], [], [], []
    for i in range(DEPTH):
        kind, j = i % N_MIXERS, i // N_MIXERS
        g = norm_g[i]
        mc = _modulation(c_ctx[None, :], w_mod[i], b_mod[i])
        ml = _modulation(c, w_mod[i], b_mod[i])
        xc = _ffn_sublayer(xc, mc, 0, g[0], g[1], ffn_w_gate[i, 0], ffn_w_up[i, 0], ffn_w_down[i, 0])
        xl = _ffn_sublayer(xl, ml, 0, g[0], g[1], ffn_w_gate[i, 0], ffn_w_up[i, 0], ffn_w_down[i, 0])
        hc = _pre(xc, g[2], mc[:, 3], mc[:, 4])
        hl = _pre(xl, g[2], ml[:, 3], ml[:, 4])
        if kind == 0:
            yc = _pool_mix(hc, pool_w[j], pool_scale[j])
            yl = _pool_mix(hl, pool_w[j], pool_scale[j])
        elif kind == 1:
            zero = (jnp.zeros((bc, MLSTM_HEADS, MLSTM_DK, MLSTM_DV), f32),
                    jnp.zeros((bc, MLSTM_HEADS, MLSTM_DK), f32),
                    jnp.zeros((bc, MLSTM_HEADS), f32))
            yc, sf, sb = _mlstm_mix(hc, mlstm_w_in[j], mlstm_b_gates[j], mlstm_w_out[j], zero, zero)
            new_C.append(jnp.stack([sf[0], sb[0]], axis=1))
            new_n.append(jnp.stack([sf[1], sb[1]], axis=1))
            new_m.append(jnp.stack([sf[2], sb[2]], axis=1))
            lat_f = (state_mlstm_C[:, j, 0].astype(f32), state_mlstm_n[:, j, 0].astype(f32),
                     state_mlstm_m[:, j, 0].astype(f32))
            lat_b = (state_mlstm_C[:, j, 1].astype(f32), state_mlstm_n[:, j, 1].astype(f32),
                     state_mlstm_m[:, j, 1].astype(f32))
            yl, _, _ = _mlstm_mix(hl, mlstm_w_in[j], mlstm_b_gates[j], mlstm_w_out[j], lat_f, lat_b)
        elif kind == 2:
            yc = _fourier_mix(hc, fourier_w_out[j])
            yl = _fourier_mix(hl, fourier_w_out[j])
        else:
            qc, kc, vc = _na_project(hc, na_w_qkv[j])
            yc = _ctx_self_attention(qc, kc, vc) @ na_w_out[j]
            new_k.append(kc)
            new_v.append(vc)
            ql, kl, vl = _na_project(hl, na_w_qkv[j])
            yl = _neighbourhood_attention(ql, kl, vl, cache_na_k[:, j], cache_na_v[:, j], na_rel_bias[j]) @ na_w_out[j]
        xc = _residual(xc, yc, g[3], mc[:, 5], MIXER_RES_WEIGHT)
        xl = _residual(xl, yl, g[3], ml[:, 5], MIXER_RES_WEIGHT)
        xc = _ffn_sublayer(xc, mc, 6, g[4], g[5], ffn_w_gate[i, 1], ffn_w_up[i, 1], ffn_w_down[i, 1])
        xl = _ffn_sublayer(xl, ml, 6, g[4], g[5], ffn_w_gate[i, 1], ffn_w_up[i, 1], ffn_w_down[i, 1])
    return (xc, xl, jnp.stack(new_k, axis=1), jnp.stack(new_v, axis=1), jnp.stack(new_C, axis=1),
            jnp.stack(new_n, axis=1), jnp.stack(new_m, axis=1))
```

```python
import functools
import math

import numpy as np
import jax
import jax.numpy as jnp
from jax import lax
from jax.experimental import pallas as pl
from jax.experimental.pallas import tpu as pltpu

F32 = jnp.float32
BF16 = jnp.bfloat16

D_MODEL = 2048
DEPTH = 4
GRID_W = 64
N_MOD = 9
D_FF = 5632
RMS_EPS = 1e-6
FFN_RES_WEIGHT = 0.5
MIXER_RES_WEIGHT = 1.0

POOL_WINDOWS = (2, 4, 8, 16)
POOL_GROUPS = len(POOL_WINDOWS)
POOL_GW = D_MODEL // POOL_GROUPS
POOL_HALO = 8

MLSTM_HEADS = 8
MLSTM_DV = D_MODEL // MLSTM_HEADS
MLSTM_DK = MLSTM_DV // 2
MLSTM_QK = MLSTM_HEADS * MLSTM_DK
MLSTM_V = MLSTM_HEADS * MLSTM_DV
MLSTM_MAIN = 2 * MLSTM_QK + 2 * MLSTM_V
MLSTM_CHUNK = 256

FOURIER_GROUPS = 4
FOURIER_GW = D_MODEL // FOURIER_GROUPS

NA_HEADS = 16
NA_HD = D_MODEL // NA_HEADS
NA_WIN_ROWS = 8
NA_WIN_COLS = 16
NA_Q_ROWS = 4
NA_K_ROWS = 12
NA_MASK = -1e30

MOD_ROWS = 16
LANES = 128
VMEM_LIMIT = 56 * 1024 * 1024


def _cparams(*sem):
    return pltpu.CompilerParams(dimension_semantics=sem, vmem_limit_bytes=VMEM_LIMIT)


def _sigmoid(x):
    return 1.0 / (1.0 + jnp.exp(-x))


def _log_sigmoid(x):
    return jnp.minimum(x, 0.0) - jnp.log(1.0 + jnp.exp(-jnp.abs(x)))


def _rms(x):
    return x * lax.rsqrt(jnp.mean(x * x, axis=-1, keepdims=True) + RMS_EPS)


def _dot(a, b):
    return jnp.dot(a, b, preferred_element_type=F32)


def _dot_nt(a, b):
    return lax.dot_general(a, b, (((1,), (1,)), ((), ())), preferred_element_type=F32)


def _mod_kernel(c_ref, w_ref, b_ref, o_ref):
    c = c_ref[...]
    s = (c * _sigmoid(c)).astype(BF16)
    o_ref[...] = _dot(s, w_ref[...].astype(BF16)) + b_ref[...]


def _modulation(cond, w_mod, b_mod, tn=1024):
    depth, d, n = w_mod.shape
    out = pl.pallas_call(
        _mod_kernel,
        out_shape=jax.ShapeDtypeStruct((depth, MOD_ROWS, n), F32),
        grid=(depth, n // tn),
        in_specs=[
            pl.BlockSpec((MOD_ROWS, d), lambda i, j: (0, 0)),
            pl.BlockSpec((None, d, tn), lambda i, j: (i, 0, j)),
            pl.BlockSpec((None, 1, tn), lambda i, j: (i, 0, j)),
        ],
        out_specs=pl.BlockSpec((None, MOD_ROWS, tn), lambda i, j: (i, 0, j)),
        compiler_params=_cparams("parallel", "parallel"),
        name="modulation",
    )(cond, w_mod, b_mod.reshape(depth, 1, n))
    return out.reshape(depth, MOD_ROWS, N_MOD, d)


def _mod_row_fn(tiles_per_seq):
    if tiles_per_seq is None:
        return lambda t: 0
    return lambda t: 1 + t // tiles_per_seq


def _ffn_kernel(x_ref, mod_ref, g_ref, wg_ref, wu_ref, wd_ref, o_ref, h_ref, *, k0, g0):
    j = pl.program_id(1)

    @pl.when(j == 0)
    def _():
        y = _rms(x_ref[...]) * g_ref[g0:g0 + 1, :]
        h = y * (1.0 + mod_ref[k0 + 1:k0 + 2, :]) + mod_ref[k0:k0 + 1, :]
        h_ref[...] = h.astype(BF16)
        o_ref[...] = jnp.zeros_like(o_ref)

    h = h_ref[...]
    a = _dot(h, wg_ref[...])
    u = _dot(h, wu_ref[...])
    act = (a * _sigmoid(a) * u).astype(BF16)
    o_ref[...] += _dot(act, wd_ref[...])

    @pl.when(j == pl.num_programs(1) - 1)
    def _():
        y = _rms(o_ref[...]) * g_ref[g0 + 1:g0 + 2, :]
        o_ref[...] = x_ref[...] + FFN_RES_WEIGHT * mod_ref[k0 + 2:k0 + 3, :] * y


def _ffn(x, mod, norm_g, wg, wu, wd, layer, sub, tiles_per_seq, tm=512, tf=512):
    n, d = x.shape
    dff = wg.shape[-1]
    row = _mod_row_fn(None if tiles_per_seq is None else tiles_per_seq // tm)
    kern = functools.partial(_ffn_kernel, k0=6 * sub, g0=4 * sub)
    return pl.pallas_call(
        kern,
        out_shape=jax.ShapeDtypeStruct((n, d), F32),
        grid=(n // tm, dff // tf),
        in_specs=[
            pl.BlockSpec((tm, d), lambda t, j: (t, 0)),
            pl.BlockSpec((None, None, N_MOD, d), lambda t, j: (layer, row(t), 0, 0)),
            pl.BlockSpec((None, 6, d), lambda t, j: (layer, 0, 0)),
            pl.BlockSpec((None, None, d, tf), lambda t, j: (layer, sub, 0, j)),
            pl.BlockSpec((None, None, d, tf), lambda t, j: (layer, sub, 0, j)),
            pl.BlockSpec((None, None, tf, d), lambda t, j: (layer, sub, j, 0)),
        ],
        out_specs=pl.BlockSpec((tm, d), lambda t, j: (t, 0)),
        scratch_shapes=[pltpu.VMEM((tm, d), BF16)],
        compiler_params=_cparams("parallel", "arbitrary"),
        name="ffn",
    )(x, mod, norm_g, wg, wu, wd)


def _prenorm_kernel(x_ref, mod_ref, g_ref, o_ref):
    y = _rms(x_ref[...]) * g_ref[2:3, :]
    o_ref[...] = (y * (1.0 + mod_ref[4:5, :]) + mod_ref[3:4, :]).astype(o_ref.dtype)


def _prenorm(x, mod, norm_g, layer, tiles_per_seq, out_dtype, tm=512):
    n, d = x.shape
    row = _mod_row_fn(None if tiles_per_seq is None else tiles_per_seq // tm)
    return pl.pallas_call(
        _prenorm_kernel,
        out_shape=jax.ShapeDtypeStruct((n, d), out_dtype),
        grid=(n // tm,),
        in_specs=[
            pl.BlockSpec((tm, d), lambda t: (t, 0)),
            pl.BlockSpec((None, None, N_MOD, d), lambda t: (layer, row(t), 0, 0)),
            pl.BlockSpec((None, 6, d), lambda t: (layer, 0, 0)),
        ],
        out_specs=pl.BlockSpec((tm, d), lambda t: (t, 0)),
        compiler_params=_cparams("parallel"),
        name="prenorm",
    )(x, mod, norm_g)


def _mm_kernel(a_ref, w_ref, o_ref):
    o_ref[...] = _dot(a_ref[...], w_ref[...]).astype(o_ref.dtype)


def _mm_bias_kernel(a_ref, w_ref, b_ref, o_ref):
    o_ref[...] = (_dot(a_ref[...], w_ref[...]) + b_ref[...]).astype(o_ref.dtype)


def _mm(a, w, out_dtype, bias=None, tm=512, tn=512):
    m, k = a.shape
    n = w.shape[1]
    tn = min(tn, n)
    tm = min(tm, m)
    in_specs = [pl.BlockSpec((tm, k), lambda i, j: (i, 0)),
                pl.BlockSpec((k, tn), lambda i, j: (0, j))]
    args = [a, w]
    kern = _mm_kernel
    if bias is not None:
        in_specs.append(pl.BlockSpec((1, tn), lambda i, j: (0, j)))
        args.append(bias.reshape(1, n))
        kern = _mm_bias_kernel
    return pl.pallas_call(
        kern,
        out_shape=jax.ShapeDtypeStruct((m, n), out_dtype),
        grid=(m // tm, n // tn),
        in_specs=in_specs,
        out_specs=pl.BlockSpec((tm, tn), lambda i, j: (i, j)),
        compiler_params=_cparams("parallel", "arbitrary"),
        name="matmul",
    )(*args)


def _proj_residual_kernel(y_ref, w_ref, x_ref, mod_ref, g_ref, o_ref):
    y = _rms(_dot(y_ref[...], w_ref[...])) * g_ref[3:4, :]
    o_ref[...] = x_ref[...] + MIXER_RES_WEIGHT * mod_ref[5:6, :] * y


def _residual_kernel(y_ref, x_ref, mod_ref, g_ref, o_ref):
    y = _rms(y_ref[...]) * g_ref[3:4, :]
    o_ref[...] = x_ref[...] + MIXER_RES_WEIGHT * mod_ref[5:6, :] * y


def _mixer_residual(y, w, x, mod, norm_g, layer, tiles_per_seq, tm=256):
    n, d = x.shape
    row = _mod_row_fn(None if tiles_per_seq is None else tiles_per_seq // tm)
    in_specs = [pl.BlockSpec((tm, y.shape[1]), lambda t: (t, 0))]
    args = [y]
    kern = _residual_kernel
    if w is not None:
        in_specs.append(pl.BlockSpec(w.shape, lambda t: (0, 0)))
        args.append(w)
        kern = _proj_residual_kernel
    in_specs += [
        pl.BlockSpec((tm, d), lambda t: (t, 0)),
        pl.BlockSpec((None, None, N_MOD, d), lambda t: (layer, row(t), 0, 0)),
        pl.BlockSpec((None, 6, d), lambda t: (layer, 0, 0)),
    ]
    args += [x, mod, norm_g]
    return pl.pallas_call(
        kern,
        out_shape=jax.ShapeDtypeStruct((n, d), F32),
        grid=(n // tm,),
        in_specs=in_specs,
        out_specs=pl.BlockSpec((tm, d), lambda t: (t, 0)),
        compiler_params=_cparams("parallel"),
        name="mixer_residual",
    )(*args)


def _pool_kernel(h_ref, w_ref, s_ref, o_ref, pad_ref, *, t_len, rows):
    g = pl.program_id(1)
    half = jnp.left_shift(1, g)
    zeros = jnp.zeros((POOL_HALO, POOL_GW), F32)
    pad_ref[0:POOL_HALO, :] = zeros
    pad_ref[POOL_HALO + t_len:, :] = zeros
    pad_ref[POOL_HALO:POOL_HALO + t_len, :] = h_ref[...]
    w = w_ref[...]
    scale = s_ref[...]

    def body(c, carry):
        r0 = pl.multiple_of(c * rows, rows)
        pos = r0 + lax.broadcasted_iota(jnp.int32, (rows, 1), 0)
        cnt = jnp.minimum(pos + half, t_len) - jnp.maximum(pos - half, 0)
        acc = jnp.zeros((rows, POOL_GW), F32)
        win = pad_ref[pl.ds(r0, rows + 2 * POOL_HALO), :]
        for off in range(-POOL_HALO, POOL_HALO):
            inside = jnp.logical_and(off >= -half, off < half)
            sh = win[POOL_HALO + off:POOL_HALO + off + rows, :]
            acc = acc + jnp.where(inside, sh, 0.0)
        centre = win[POOL_HALO:POOL_HALO + rows, :]
        dlt = acc / cnt.astype(F32) - centre
        o_ref[pl.ds(r0, rows), :] = _dot(dlt.astype(BF16), w) * scale
        return carry

    lax.fori_loop(0, t_len // rows, body, 0)


def _pool_mix(h, w, scale, rows=128):
    b, t, d = h.shape
    kern = functools.partial(_pool_kernel, t_len=t, rows=rows)
    return pl.pallas_call(
        kern,
        out_shape=jax.ShapeDtypeStruct((b, t, d), F32),
        grid=(b, POOL_GROUPS),
        in_specs=[
            pl.BlockSpec((None, t, POOL_GW), lambda i, g: (i, 0, g)),
            pl.BlockSpec((None, POOL_GW, POOL_GW), lambda i, g: (g, 0, 0)),
            pl.BlockSpec((1, POOL_GW), lambda i, g: (0, g)),
        ],
        out_specs=pl.BlockSpec((None, t, POOL_GW), lambda i, g: (i, 0, g)),
        scratch_shapes=[pltpu.VMEM((t + 2 * POOL_HALO, POOL_GW), F32)],
        compiler_params=_cparams("parallel", "arbitrary"),
        name="pool_mix",
    )(h, w, scale.reshape(1, d))


def _mlstm_kernel(*refs, t_len, chunk, has_init, emit_state):
    q_ref, k_ref, v_ref, og_ref, gc_ref, gr_ref = refs[:6]
    refs = refs[6:]
    if has_init:
        c0_ref, n0_ref, m0_ref = refs[:3]
        refs = refs[3:]
    out_ref = refs[0]
    refs = refs[1:]
    if emit_state:
        co_ref, no_ref, mo_ref = refs[:3]
        refs = refs[3:]
    hsum_ref, c_scr, n_scr, m_scr = refs

    nc = t_len // chunk
    qscale = MLSTM_DK ** -0.5

    if has_init:
        c_scr[...] = c0_ref[...]
        n_scr[...] = n0_ref[...]
        m_scr[...] = m0_ref[...]
    else:
        c_scr[...] = jnp.zeros_like(c_scr)
        n_scr[...] = jnp.zeros_like(n_scr)
        m_scr[...] = jnp.zeros_like(m_scr)

    tt = lax.broadcasted_iota(jnp.int32, (chunk, chunk), 0)
    ss = lax.broadcasted_iota(jnp.int32, (chunk, chunk), 1)

    def run_chunk(c, rev):
        d = 1 if rev else 0
        r0 = pl.multiple_of(c * chunk, chunk)
        q = q_ref[pl.ds(r0, chunk), :]
        k = k_ref[pl.ds(r0, chunk), :]
        v = v_ref[pl.ds(r0, chunk), :]
        gcol = gc_ref[pl.ds(r0, chunk), :]
        grow = gr_ref[:, pl.ds(r0, chunk)]
        i_col = gcol[:, 2 * d:2 * d + 1]
        f_col = _log_sigmoid(gcol[:, 2 * d + 1:2 * d + 2])
        i_row = grow[2 * d:2 * d + 1, :]
        f_row = _log_sigmoid(grow[2 * d + 1:2 * d + 2, :])
        incl = (ss >= tt) if rev else (ss <= tt)
        incl_t = (tt >= ss) if rev else (tt <= ss)
        bcum_col = jnp.sum(jnp.where(incl, f_row, 0.0), axis=1, keepdims=True)
        bcum_row = jnp.sum(jnp.where(incl_t, f_col, 0.0), axis=0, keepdims=True)
        dmat = jnp.where(incl, bcum_col - bcum_row + i_row, -jnp.inf)
        m_prev = m_scr[d][:, 0:1]
        inter = bcum_col + m_prev
        mt = jnp.maximum(inter, jnp.max(dmat, axis=1, keepdims=True))
        w_inter = jnp.exp(inter - mt)
        s = _dot_nt(q, k) * (qscale * jnp.where(incl, jnp.exp(dmat - mt), 0.0))
        c_prev = c_scr[d]
        n_prev = n_scr[d]
        num = w_inter * (_dot(q, c_prev.astype(BF16)) * qscale) + _dot(s.astype(BF16), v)
        qn = jnp.sum(q.astype(F32) * n_prev, axis=1, keepdims=True) * qscale
        den = w_inter * qn + jnp.sum(s, axis=1, keepdims=True)
        h = num / jnp.maximum(jnp.abs(den), jnp.exp(-mt))
        last = 0 if rev else chunk - 1
        m_new = mt[last:last + 1, :]
        b_last = bcum_col[last:last + 1, :]
        decay = jnp.exp(b_last + m_prev - m_new)
        ws = jnp.exp(b_last - bcum_col + i_col - m_new)
        kw = k.astype(F32) * ws
        c_scr[d] = decay * c_prev + _dot(kw.T.astype(BF16), v)
        n_scr[d] = decay * n_prev + jnp.sum(kw, axis=0, keepdims=True)
        m_scr[d] = jnp.broadcast_to(m_new, (1, LANES))
        return r0, h

    def fwd(i, carry):
        r0, h = run_chunk(i, False)
        hsum_ref[pl.ds(r0, chunk), :] = h
        return carry

    def bwd(i, carry):
        r0, h = run_chunk(nc - 1 - i, True)
        gate = _sigmoid(og_ref[pl.ds(r0, chunk), :].astype(F32))
        out_ref[pl.ds(r0, chunk), :] = (gate * (hsum_ref[pl.ds(r0, chunk), :] + h)).astype(out_ref.dtype)
        return carry

    lax.fori_loop(0, nc, fwd, 0)
    lax.fori_loop(0, nc, bwd, 0)

    if emit_state:
        co_ref[...] = c_scr[...]
        no_ref[...] = n_scr[...]
        mo_ref[...] = m_scr[...]


def _mlstm_core(p, gates, init, emit_state):
    b, t, _ = p.shape
    nh, dk, dv = MLSTM_HEADS, MLSTM_DK, MLSTM_DV
    chunk = min(MLSTM_CHUNK, t)
    g4 = gates.reshape(b, t, 4, nh)
    gcol = g4.transpose(0, 3, 1, 2)
    grow = g4.transpose(0, 3, 2, 1)
    args = [p, p, p, p, gcol, grow]
    in_specs = [
        pl.BlockSpec((None, t, dk), lambda i, h: (i, 0, h)),
        pl.BlockSpec((None, t, dk), lambda i, h: (i, 0, nh + h)),
        pl.BlockSpec((None, t, dv), lambda i, h: (i, 0, nh + h)),
        pl.BlockSpec((None, t, dv), lambda i, h: (i, 0, 2 * nh + h)),
        pl.BlockSpec((None, None, t, 4), lambda i, h: (i, h, 0, 0)),
        pl.BlockSpec((None, None, 4, t), lambda i, h: (i, h, 0, 0)),
    ]
    has_init = init is not None
    if has_init:
        c0, n0, m0 = init
        n0 = n0.transpose(0, 2, 1, 3).reshape(b, nh, 2, 1, dk)
        m0 = jnp.broadcast_to(m0.transpose(0, 2, 1)[..., None, None], (b, nh, 2, 1, LANES))
        args += [c0, n0, m0]
        in_specs += [
            pl.BlockSpec((None, 2, None, dk, dv), lambda i, h: (i, 0, h, 0, 0)),
            pl.BlockSpec((None, None, 2, 1, dk), lambda i, h: (i, h, 0, 0, 0)),
            pl.BlockSpec((None, None, 2, 1, LANES), lambda i, h: (i, h, 0, 0, 0)),
        ]
    out_shape = [jax.ShapeDtypeStruct((b, t, MLSTM_V), BF16)]
    out_specs = [pl.BlockSpec((None, t, dv), lambda i, h: (i, 0, h))]
    if emit_state:
        out_shape += [
            jax.ShapeDtypeStruct((b, 2, nh, dk, dv), F32),
            jax.ShapeDtypeStruct((b, nh, 2, 1, dk), F32),
            jax.ShapeDtypeStruct((b, nh, 2, 1, LANES), F32),
        ]
        out_specs += [
            pl.BlockSpec((None, 2, None, dk, dv), lambda i, h: (i, 0, h, 0, 0)),
            pl.BlockSpec((None, None, 2, 1, dk), lambda i, h: (i, h, 0, 0, 0)),
            pl.BlockSpec((None, None, 2, 1, LANES), lambda i, h: (i, h, 0, 0, 0)),
        ]
    kern = functools.partial(_mlstm_kernel, t_len=t, chunk=chunk, has_init=has_init,
                             emit_state=emit_state)
    outs = pl.pallas_call(
        kern,
        out_shape=out_shape,
        grid=(b, nh),
        in_specs=in_specs,
        out_specs=out_specs,
        scratch_shapes=[
            pltpu.VMEM((t, dv), F32),
            pltpu.VMEM((2, dk, dv), F32),
            pltpu.VMEM((2, 1, dk), F32),
            pltpu.VMEM((2, 1, LANES), F32),
        ],
        compiler_params=_cparams("parallel", "parallel"),
        name="mlstm",
    )(*args)
    if not emit_state:
        return outs[0], None
    hs, c_fin, n_fin, m_fin = outs
    n_fin = n_fin.reshape(b, nh, 2, dk).transpose(0, 2, 1, 3)
    m_fin = m_fin[:, :, :, 0, 0].transpose(0, 2, 1)
    return hs, (c_fin, n_fin, m_fin)


def _mlstm_mix(h, w_main, w_gate, b_gate, batch, init, emit_state):
    n, _ = h.shape
    t = n // batch
    p = _mm(h, w_main, BF16)
    gates = _mm(h, w_gate, F32, bias=b_gate)[:, :4 * MLSTM_HEADS]
    hs, state = _mlstm_core(p.reshape(batch, t, MLSTM_MAIN), gates.reshape(batch, t, 4 * MLSTM_HEADS),
                            init, emit_state)
    return hs.reshape(n, MLSTM_V), state


def _dft_tables(n, norm):
    idx = np.arange(n, dtype=np.int64)
    ang = 2.0 * np.pi * ((idx[:, None] * idx[None, :]) % n).astype(np.float64) / n
    return np.cos(ang) * norm, np.sin(ang) * norm


def _dft_cols_kernel(h_ref, cs_ref, o_ref):
    u = _dot(h_ref[...], cs_ref[...])
    o_ref[0] = u[:, :FOURIER_GW].astype(o_ref.dtype)
    o_ref[1] = u[:, FOURIER_GW:].astype(o_ref.dtype)


def _dft_cols(h, tm=512):
    n, d = h.shape
    gw = FOURIER_GW
    cos_c, sin_c = _dft_tables(gw, gw ** -0.5)
    cs = jnp.asarray(np.concatenate([cos_c, sin_c], axis=1), BF16)
    return pl.pallas_call(
        _dft_cols_kernel,
        out_shape=jax.ShapeDtypeStruct((2, n, d), BF16),
        grid=(n // tm, FOURIER_GROUPS),
        in_specs=[
            pl.BlockSpec((tm, gw), lambda i, g: (i, g)),
            pl.BlockSpec((gw, 2 * gw), lambda i, g: (0, 0)),
        ],
        out_specs=pl.BlockSpec((2, tm, gw), lambda i, g: (0, i, g)),
        compiler_params=_cparams("parallel", "arbitrary"),
        name="dft_cols",
    )(h, cs)


def _dft_rows_kernel(ct_ref, st_ref, uc_ref, us_ref, o_ref):
    o_ref[...] = (_dot(ct_ref[...], uc_ref[...]) + _dot(st_ref[...], us_ref[...])).astype(o_ref.dtype)


def _dft_rows(u, batch, tm=512, tn=512):
    _, n, d = u.shape
    t = n // batch
    tm = min(tm, t)
    cos_t, sin_t = _dft_tables(t, t ** -0.5)
    ct = jnp.asarray(cos_t, BF16)
    st = jnp.asarray(-sin_t, BF16)
    u4 = u.reshape(2, batch, t, d)
    out = pl.pallas_call(
        _dft_rows_kernel,
        out_shape=jax.ShapeDtypeStruct((batch, t, d), BF16),
        grid=(batch, d // tn, t // tm),
        in_specs=[
            pl.BlockSpec((tm, t), lambda b, j, i: (i, 0)),
            pl.BlockSpec((tm, t), lambda b, j, i: (i, 0)),
            pl.BlockSpec((None, None, t, tn), lambda b, j, i: (0, b, 0, j)),
            pl.BlockSpec((None, None, t, tn), lambda b, j, i: (1, b, 0, j)),
        ],
        out_specs=pl.BlockSpec((None, tm, tn), lambda b, j, i: (b, i, j)),
        compiler_params=_cparams("parallel", "parallel", "arbitrary"),
        name="dft_rows",
    )(ct, st, u4, u4)
    return out.reshape(n, d)


def _ctx_attn_kernel(q_ref, k_ref, v_ref, o_ref):
    q = q_ref[...].astype(BF16)
    k = k_ref[...].astype(BF16)
    v = v_ref[...].astype(BF16)
    s = _dot_nt(q, k) * (NA_HD ** -0.5)
    p = jnp.exp(s - jnp.max(s, axis=1, keepdims=True))
    l = jnp.sum(p, axis=1, keepdims=True)
    o_ref[...] = (_dot(p.astype(BF16), v) / l).astype(o_ref.dtype)


def _ctx_attention(qkv):
    b, s, _ = qkv.shape
    nh, dh = NA_HEADS, NA_HD
    return pl.pallas_call(
        _ctx_attn_kernel,
        out_shape=jax.ShapeDtypeStruct((b, s, D_MODEL), BF16),
        grid=(b, nh),
        in_specs=[
            pl.BlockSpec((None, s, dh), lambda i, h: (i, 0, h)),
            pl.BlockSpec((None, s, dh), lambda i, h: (i, 0, nh + h)),
            pl.BlockSpec((None, s, dh), lambda i, h: (i, 0, 2 * nh + h)),
        ],
        out_specs=pl.BlockSpec((None, s, dh), lambda i, h: (i, 0, h)),
        compiler_params=_cparams("parallel", "parallel"),
        name="ctx_attention",
    )(qkv, qkv, qkv)


def _na_bias_table(rel_bias, rows):
    w = GRID_W
    col = np.arange(w)
    col_start = np.clip(col - NA_WIN_COLS // 2, 0, w - NA_WIN_COLS)
    col_mask = (col[None, :] >= col_start[:, None]) & (col[None, :] < col_start[:, None] + NA_WIN_COLS)
    col_idx = np.clip(col[None, :] - col[:, None] + NA_WIN_COLS - 1, 0, 2 * NA_WIN_COLS - 2)
    nh = rel_bias.shape[0]
    bias_cols = jnp.where(col_mask[None, None], rel_bias[:, :, col_idx].astype(F32), NA_MASK)
    masked = jnp.full((nh, 1, w, w), NA_MASK, F32)
    planes = jnp.concatenate([bias_cols, masked], axis=1)
    n_blk = rows // NA_Q_ROWS
    plane_idx = np.empty((n_blk, NA_Q_ROWS, NA_K_ROWS), np.int32)
    for rb in range(n_blk):
        k0 = _na_key_row_start(rb, rows)
        for a in range(NA_Q_ROWS):
            r = rb * NA_Q_ROWS + a
            sr = min(max(r - NA_WIN_ROWS // 2, 0), rows - NA_WIN_ROWS)
            for kk in range(NA_K_ROWS):
                kr = k0 + kk
                inside = sr <= kr < sr + NA_WIN_ROWS
                plane_idx[rb, a, kk] = kr - r + NA_WIN_ROWS - 1 if inside else 2 * NA_WIN_ROWS - 1
    tab = planes[:, plane_idx.reshape(-1)]
    tab = tab.reshape(nh, n_blk, NA_Q_ROWS, NA_K_ROWS, w, w).transpose(0, 1, 2, 4, 3, 5)
    return tab.reshape(nh, n_blk, NA_Q_ROWS * w, NA_K_ROWS * w)


def _na_key_row_start(rb, rows):
    lo = rb * NA_Q_ROWS - NA_WIN_ROWS // 2
    return min(max(lo, 0), rows - NA_K_ROWS)


def _na_kernel(q_ref, k_ref, v_ref, kc_ref, vc_ref, b_ref, o_ref, *, rows):
    rb = pl.program_id(2)
    scale = NA_HD ** -0.5
    lo = rb * NA_Q_ROWS - NA_WIN_ROWS // 2
    start = pl.multiple_of(jnp.clip(lo, 0, rows - NA_K_ROWS) * GRID_W, GRID_W)
    nk = NA_K_ROWS * GRID_W
    q = q_ref[...]
    kw = k_ref[pl.ds(start, nk), :]
    vw = v_ref[pl.ds(start, nk), :]
    s_loc = _dot_nt(q, kw) * scale + b_ref[rb]
    s_ctx = _dot_nt(q, kc_ref[...]) * scale
    m = jnp.maximum(jnp.max(s_loc, axis=1, keepdims=True), jnp.max(s_ctx, axis=1, keepdims=True))
    p_loc = jnp.exp(s_loc - m)
    p_ctx = jnp.exp(s_ctx - m)
    l = jnp.sum(p_loc, axis=1, keepdims=True) + jnp.sum(p_ctx, axis=1, keepdims=True)
    o = _dot(p_loc.astype(BF16), vw) + _dot(p_ctx.astype(BF16), vc_ref[...])
    o_ref[...] = (o / l).astype(o_ref.dtype)


def _neighbourhood_attention(qkv, k_ctx, v_ctx, rel_bias):
    b, t, _ = qkv.shape
    nh, dh = NA_HEADS, NA_HD
    rows = t // GRID_W
    n_blk = rows // NA_Q_ROWS
    tq = NA_Q_ROWS * GRID_W
    tk = NA_K_ROWS * GRID_W
    past = k_ctx.shape[2]
    table = _na_bias_table(rel_bias, rows)
    kern = functools.partial(_na_kernel, rows=rows)
    return pl.pallas_call(
        kern,
        out_shape=jax.ShapeDtypeStruct((b, t, D_MODEL), BF16),
        grid=(nh, b, n_blk),
        in_specs=[
            pl.BlockSpec((None, tq, dh), lambda h, i, r: (i, r, h)),
            pl.BlockSpec((None, t, dh), lambda h, i, r: (i, 0, nh + h)),
            pl.BlockSpec((None, t, dh), lambda h, i, r: (i, 0, 2 * nh + h)),
            pl.BlockSpec((None, None, past, dh), lambda h, i, r: (i, h, 0, 0)),
            pl.BlockSpec((None, None, past, dh), lambda h, i, r: (i, h, 0, 0)),
            pl.BlockSpec((None, n_blk, tq, tk), lambda h, i, r: (h, 0, 0, 0)),
        ],
        out_specs=pl.BlockSpec((None, tq, dh), lambda h, i, r: (i, r, h)),
        compiler_params=_cparams("parallel", "parallel", "arbitrary"),
        name="neighbourhood_attention",
    )(qkv, qkv, qkv, k_ctx, v_ctx, table)


def kernel(x_prompt, x_sample, cache_na_k, cache_na_v, state_mlstm_C, state_mlstm_n, state_mlstm_m,
           c, c_ctx, w_mod, b_mod, norm_g, ffn_w_gate, ffn_w_up, ffn_w_down, pool_w, pool_scale,
           mlstm_w_in, mlstm_b_gates, mlstm_w_out, fourier_w_out, na_w_qkv, na_rel_bias, na_w_out):
    bc, sc, d = x_prompt.shape
    bl, sl, _ = x_sample.shape
    xc = x_prompt.reshape(bc * sc, d)
    xl = x_sample.reshape(bl * sl, d)

    cond = jnp.concatenate([c_ctx[None, :], c, jnp.zeros((MOD_ROWS - 1 - bl, d), F32)], axis=0)
    mod = _modulation(cond, w_mod, b_mod)

    wg = ffn_w_gate.astype(BF16)
    wu = ffn_w_up.astype(BF16)
    wd = ffn_w_down.astype(BF16)

    new_k = new_v = new_state = None
    for i in range(DEPTH):
        kind, j = i % 4, i // 4
        xc = _ffn(xc, mod, norm_g, wg, wu, wd, i, 0, None)
        xl = _ffn(xl, mod, norm_g, wg, wu, wd, i, 0, sl)
        if kind == 0:
            pw = pool_w[j].astype(BF16)
            hc = _prenorm(xc, mod, norm_g, i, None, F32)
            hl = _prenorm(xl, mod, norm_g, i, sl, F32)
            yc = _pool_mix(hc.reshape(bc, sc, d), pw, pool_scale[j]).reshape(bc * sc, d)
            yl = _pool_mix(hl.reshape(bl, sl, d), pw, pool_scale[j]).reshape(bl * sl, d)
            w_out = None
        elif kind == 1:
            hc = _prenorm(xc, mod, norm_g, i, None, BF16)
            hl = _prenorm(xl, mod, norm_g, i, sl, BF16)
            w_in = mlstm_w_in[j]
            w_main = w_in[:, :MLSTM_MAIN].astype(BF16)
            n_gate = 4 * MLSTM_HEADS
            w_gate = jnp.pad(w_in[:, MLSTM_MAIN:], ((0, 0), (0, LANES - n_gate))).astype(BF16)
            b_gate = jnp.pad(mlstm_b_gates[j].astype(F32), (0, LANES - n_gate))
            yc, new_state = _mlstm_mix(hc, w_main, w_gate, b_gate, bc, None, True)
            init = (state_mlstm_C[:, j].astype(F32), state_mlstm_n[:, j].astype(F32),
                    state_mlstm_m[:, j].astype(F32))
            yl, _ = _mlstm_mix(hl, w_main, w_gate, b_gate, bl, init, False)
            w_out = mlstm_w_out[j].astype(BF16)
        elif kind == 2:
            hc = _prenorm(xc, mod, norm_g, i, None, BF16)
            hl = _prenorm(xl, mod, norm_g, i, sl, BF16)
            yc = _dft_rows(_dft_cols(hc), bc)
            yl = _dft_rows(_dft_cols(hl), bl)
            w_out = fourier_w_out[j].astype(BF16)
        else:
            hc = _prenorm(xc, mod, norm_g, i, None, BF16)
            hl = _prenorm(xl, mod, norm_g, i, sl, BF16)
            w_qkv = na_w_qkv[j].astype(BF16)
            qkv_c = _mm(hc, w_qkv, F32).reshape(bc, sc, 3 * d)
            new_k = qkv_c[:, :, d:2 * d].reshape(bc, 1, sc, NA_HEADS, NA_HD)
            new_v = qkv_c[:, :, 2 * d:].reshape(bc, 1, sc, NA_HEADS, NA_HD)
            yc = _ctx_attention(qkv_c).reshape(bc * sc, d)
            qkv_l = _mm(hl, w_qkv, BF16).reshape(bl, sl, 3 * d)
            k_ctx = cache_na_k[:, j].transpose(0, 2, 1, 3).astype(BF16)
            v_ctx = cache_na_v[:, j].transpose(0, 2, 1, 3).astype(BF16)
            yl = _neighbourhood_attention(qkv_l, k_ctx, v_ctx, na_rel_bias[j]).reshape(bl * sl, d)
            w_out = na_w_out[j].astype(BF16)
        xc = _mixer_residual(yc, w_out, xc, mod, norm_g, i, None)
        xl = _mixer_residual(yl, w_out, xl, mod, norm_g, i, sl)
        xc = _ffn(xc, mod, norm_g, wg, wu, wd, i, 1, None)
        xl = _ffn(xl, mod, norm_g, wg, wu, wd, i, 1, sl)

    c_fin, n_fin, m_fin = new_state
    return (xc.reshape(bc, sc, d), xl.reshape(bl, sl, d), new_k, new_v,
            c_fin[:, None], n_fin[:, None], m_fin[:, None])
```

```python
import functools
import math

import numpy as np
import jax
import jax.numpy as jnp
from jax import lax
from jax.experimental import pallas as pl
from jax.experimental.pallas import tpu as pltpu

F32 = jnp.float32
BF16 = jnp.bfloat16

D_MODEL = 2048
DEPTH = 4
GRID_W = 64
N_MOD = 9
D_FF = 5632
RMS_EPS = 1e-6
FFN_RES_WEIGHT = 0.5
MIXER_RES_WEIGHT = 1.0

POOL_WINDOWS = (2, 4, 8, 16)
POOL_GROUPS = len(POOL_WINDOWS)
POOL_GW = D_MODEL // POOL_GROUPS
POOL_HALO = 8

MLSTM_HEADS = 8
MLSTM_DV = D_MODEL // MLSTM_HEADS
MLSTM_DK = MLSTM_DV // 2
MLSTM_QK = MLSTM_HEADS * MLSTM_DK
MLSTM_V = MLSTM_HEADS * MLSTM_DV
MLSTM_MAIN = 2 * MLSTM_QK + 2 * MLSTM_V
MLSTM_CHUNK = 256

FOURIER_GROUPS = 4
FOURIER_GW = D_MODEL // FOURIER_GROUPS

NA_HEADS = 16
NA_HD = D_MODEL // NA_HEADS
NA_WIN_ROWS = 8
NA_WIN_COLS = 16
NA_Q_ROWS = 4
NA_K_ROWS = 12
NA_MASK = -1e30

MOD_ROWS = 16
LANES = 128
VMEM_LIMIT = 56 * 1024 * 1024


def _cparams(*sem):
    return pltpu.CompilerParams(dimension_semantics=sem, vmem_limit_bytes=VMEM_LIMIT)


def _sigmoid(x):
    return 1.0 / (1.0 + jnp.exp(-x))


def _log_sigmoid(x):
    return jnp.minimum(x, 0.0) - jnp.log(1.0 + jnp.exp(-jnp.abs(x)))


def _rms(x):
    return x * lax.rsqrt(jnp.mean(x * x, axis=-1, keepdims=True) + RMS_EPS)


def _dot(a, b):
    return jnp.dot(a, b, preferred_element_type=F32)


def _dot_nt(a, b):
    return lax.dot_general(a, b, (((1,), (1,)), ((), ())), preferred_element_type=F32)


def _mod_kernel(c_ref, w_ref, b_ref, o_ref):
    c = c_ref[...]
    s = (c * _sigmoid(c)).astype(BF16)
    o_ref[...] = _dot(s, w_ref[...].astype(BF16)) + b_ref[...]


def _modulation(cond, w_mod, b_mod, tn=1024):
    depth, d, n = w_mod.shape
    out = pl.pallas_call(
        _mod_kernel,
        out_shape=jax.ShapeDtypeStruct((depth, MOD_ROWS, n), F32),
        grid=(depth, n // tn),
        in_specs=[
            pl.BlockSpec((MOD_ROWS, d), lambda i, j: (0, 0)),
            pl.BlockSpec((None, d, tn), lambda i, j: (i, 0, j)),
            pl.BlockSpec((None, 1, tn), lambda i, j: (i, 0, j)),
        ],
        out_specs=pl.BlockSpec((None, MOD_ROWS, tn), lambda i, j: (i, 0, j)),
        compiler_params=_cparams("parallel", "parallel"),
        name="modulation",
    )(cond, w_mod, b_mod.reshape(depth, 1, n))
    return out.reshape(depth, MOD_ROWS, N_MOD, d)


def _mod_row_fn(tiles_per_seq):
    if tiles_per_seq is None:
        return lambda t: 0
    return lambda t: 1 + t // tiles_per_seq


def _ffn_kernel(x_ref, mod_ref, g_ref, wg_ref, wu_ref, wd_ref, o_ref, h_ref, *, k0, g0):
    j = pl.program_id(1)

    @pl.when(j == 0)
    def _():
        y = _rms(x_ref[...]) * g_ref[g0:g0 + 1, :]
        h = y * (1.0 + mod_ref[k0 + 1:k0 + 2, :]) + mod_ref[k0:k0 + 1, :]
        h_ref[...] = h.astype(BF16)
        o_ref[...] = jnp.zeros_like(o_ref)

    h = h_ref[...]
    a = _dot(h, wg_ref[...])
    u = _dot(h, wu_ref[...])
    act = (a * _sigmoid(a) * u).astype(BF16)
    o_ref[...] += _dot(act, wd_ref[...])

    @pl.when(j == pl.num_programs(1) - 1)
    def _():
        y = _rms(o_ref[...]) * g_ref[g0 + 1:g0 + 2, :]
        o_ref[...] = x_ref[...] + FFN_RES_WEIGHT * mod_ref[k0 + 2:k0 + 3, :] * y


def _ffn(x, mod, norm_g, wg, wu, wd, layer, sub, tiles_per_seq, tm=512, tf=512):
    n, d = x.shape
    dff = wg.shape[-1]
    row = _mod_row_fn(None if tiles_per_seq is None else tiles_per_seq // tm)
    kern = functools.partial(_ffn_kernel, k0=6 * sub, g0=4 * sub)
    return pl.pallas_call(
        kern,
        out_shape=jax.ShapeDtypeStruct((n, d), F32),
        grid=(n // tm, dff // tf),
        in_specs=[
            pl.BlockSpec((tm, d), lambda t, j: (t, 0)),
            pl.BlockSpec((None, None, N_MOD, d), lambda t, j: (layer, row(t), 0, 0)),
            pl.BlockSpec((None, 6, d), lambda t, j: (layer, 0, 0)),
            pl.BlockSpec((None, None, d, tf), lambda t, j: (layer, sub, 0, j)),
            pl.BlockSpec((None, None, d, tf), lambda t, j: (layer, sub, 0, j)),
            pl.BlockSpec((None, None, tf, d), lambda t, j: (layer, sub, j, 0)),
        ],
        out_specs=pl.BlockSpec((tm, d), lambda t, j: (t, 0)),
        scratch_shapes=[pltpu.VMEM((tm, d), BF16)],
        compiler_params=_cparams("parallel", "arbitrary"),
        name="ffn",
    )(x, mod, norm_g, wg, wu, wd)


def _prenorm_kernel(x_ref, mod_ref, g_ref, o_ref):
    y = _rms(x_ref[...]) * g_ref[2:3, :]
    o_ref[...] = (y * (1.0 + mod_ref[4:5, :]) + mod_ref[3:4, :]).astype(o_ref.dtype)


def _prenorm(x, mod, norm_g, layer, tiles_per_seq, out_dtype, tm=512):
    n, d = x.shape
    row = _mod_row_fn(None if tiles_per_seq is None else tiles_per_seq // tm)
    return pl.pallas_call(
        _prenorm_kernel,
        out_shape=jax.ShapeDtypeStruct((n, d), out_dtype),
        grid=(n // tm,),
        in_specs=[
            pl.BlockSpec((tm, d), lambda t: (t, 0)),
            pl.BlockSpec((None, None, N_MOD, d), lambda t: (layer, row(t), 0, 0)),
            pl.BlockSpec((None, 6, d), lambda t: (layer, 0, 0)),
        ],
        out_specs=pl.BlockSpec((tm, d), lambda t: (t, 0)),
        compiler_params=_cparams("parallel"),
        name="prenorm",
    )(x, mod, norm_g)


def _mm_kernel(a_ref, w_ref, o_ref):
    o_ref[...] = _dot(a_ref[...], w_ref[...]).astype(o_ref.dtype)


def _mm_bias_kernel(a_ref, w_ref, b_ref, o_ref):
    o_ref[...] = (_dot(a_ref[...], w_ref[...]) + b_ref[...]).astype(o_ref.dtype)


def _mm(a, w, out_dtype, bias=None, tm=512, tn=512):
    m, k = a.shape
    n = w.shape[1]
    tn = min(tn, n)
    tm = min(tm, m)
    in_specs = [pl.BlockSpec((tm, k), lambda i, j: (i, 0)),
                pl.BlockSpec((k, tn), lambda i, j: (0, j))]
    args = [a, w]
    kern = _mm_kernel
    if bias is not None:
        in_specs.append(pl.BlockSpec((1, tn), lambda i, j: (0, j)))
        args.append(bias.reshape(1, n))
        kern = _mm_bias_kernel
    return pl.pallas_call(
        kern,
        out_shape=jax.ShapeDtypeStruct((m, n), out_dtype),
        grid=(m // tm, n // tn),
        in_specs=in_specs,
        out_specs=pl.BlockSpec((tm, tn), lambda i, j: (i, j)),
        compiler_params=_cparams("parallel", "arbitrary"),
        name="matmul",
    )(*args)


def _proj_residual_kernel(y_ref, w_ref, x_ref, mod_ref, g_ref, o_ref):
    y = _rms(_dot(y_ref[...], w_ref[...])) * g_ref[3:4, :]
    o_ref[...] = x_ref[...] + MIXER_RES_WEIGHT * mod_ref[5:6, :] * y


def _residual_kernel(y_ref, x_ref, mod_ref, g_ref, o_ref):
    y = _rms(y_ref[...]) * g_ref[3:4, :]
    o_ref[...] = x_ref[...] + MIXER_RES_WEIGHT * mod_ref[5:6, :] * y


def _mixer_residual(y, w, x, mod, norm_g, layer, tiles_per_seq, tm=256):
    n, d = x.shape
    row = _mod_row_fn(None if tiles_per_seq is None else tiles_per_seq // tm)
    in_specs = [pl.BlockSpec((tm, y.shape[1]), lambda t: (t, 0))]
    args = [y]
    kern = _residual_kernel
    if w is not None:
        in_specs.append(pl.BlockSpec(w.shape, lambda t: (0, 0)))
        args.append(w)
        kern = _proj_residual_kernel
    in_specs += [
        pl.BlockSpec((tm, d), lambda t: (t, 0)),
        pl.BlockSpec((None, None, N_MOD, d), lambda t: (layer, row(t), 0, 0)),
        pl.BlockSpec((None, 6, d), lambda t: (layer, 0, 0)),
    ]
    args += [x, mod, norm_g]
    return pl.pallas_call(
        kern,
        out_shape=jax.ShapeDtypeStruct((n, d), F32),
        grid=(n // tm,),
        in_specs=in_specs,
        out_specs=pl.BlockSpec((tm, d), lambda t: (t, 0)),
        compiler_params=_cparams("parallel"),
        name="mixer_residual",
    )(*args)


def _pool_kernel(h_ref, w_ref, s_ref, o_ref, pad_ref, *, t_len, rows):
    g = pl.program_id(1)
    half = jnp.left_shift(1, g)
    zeros = jnp.zeros((POOL_HALO, POOL_GW), F32)
    pad_ref[0:POOL_HALO, :] = zeros
    pad_ref[POOL_HALO + t_len:, :] = zeros
    pad_ref[POOL_HALO:POOL_HALO + t_len, :] = h_ref[...]
    w = w_ref[...]
    scale = s_ref[...]

    def body(c, carry):
        r0 = pl.multiple_of(c * rows, rows)
        pos = r0 + lax.broadcasted_iota(jnp.int32, (rows, 1), 0)
        cnt = jnp.minimum(pos + half, t_len) - jnp.maximum(pos - half, 0)
        acc = jnp.zeros((rows, POOL_GW), F32)
        win = pad_ref[pl.ds(r0, rows + 2 * POOL_HALO), :]
        for off in range(-POOL_HALO, POOL_HALO):
            inside = jnp.logical_and(off >= -half, off < half)
            sh = win[POOL_HALO + off:POOL_HALO + off + rows, :]
            acc = acc + jnp.where(inside, sh, 0.0)
        centre = win[POOL_HALO:POOL_HALO + rows, :]
        dlt = acc / cnt.astype(F32) - centre
        o_ref[pl.ds(r0, rows), :] = _dot(dlt.astype(BF16), w) * scale
        return carry

    lax.fori_loop(0, t_len // rows, body, 0)


def _pool_mix(h, w, scale, rows=128):
    b, t, d = h.shape
    kern = functools.partial(_pool_kernel, t_len=t, rows=rows)
    return pl.pallas_call(
        kern,
        out_shape=jax.ShapeDtypeStruct((b, t, d), F32),
        grid=(b, POOL_GROUPS),
        in_specs=[
            pl.BlockSpec((None, t, POOL_GW), lambda i, g: (i, 0, g)),
            pl.BlockSpec((None, POOL_GW, POOL_GW), lambda i, g: (g, 0, 0)),
            pl.BlockSpec((1, POOL_GW), lambda i, g: (0, g)),
        ],
        out_specs=pl.BlockSpec((None, t, POOL_GW), lambda i, g: (i, 0, g)),
        scratch_shapes=[pltpu.VMEM((t + 2 * POOL_HALO, POOL_GW), F32)],
        compiler_params=_cparams("parallel", "arbitrary"),
        name="pool_mix",
    )(h, w, scale.reshape(1, d))


def _mlstm_kernel(*refs, t_len, chunk, has_init, emit_state):
    q_ref, k_ref, v_ref, og_ref, gc_ref, gr_ref = refs[:6]
    refs = refs[6:]
    if has_init:
        c0_ref, n0_ref, m0_ref = refs[:3]
        refs = refs[3:]
    out_ref = refs[0]
    refs = refs[1:]
    if emit_state:
        co_ref, no_ref, mo_ref = refs[:3]
        refs = refs[3:]
    hsum_ref, hb_ref, c_scr, n_scr, m_scr = refs

    nc = t_len // chunk
    qscale = MLSTM_DK ** -0.5

    if has_init:
        c_scr[...] = c0_ref[...]
        n_scr[...] = n0_ref[...]
        m_scr[...] = m0_ref[...]
    else:
        c_scr[...] = jnp.zeros_like(c_scr)
        n_scr[...] = jnp.zeros_like(n_scr)
        m_scr[...] = jnp.zeros_like(m_scr)

    tt = lax.broadcasted_iota(jnp.int32, (chunk, chunk), 0)
    ss = lax.broadcasted_iota(jnp.int32, (chunk, chunk), 1)

    def run_chunk(c, rev):
        d = 1 if rev else 0
        r0 = pl.multiple_of(c * chunk, chunk)
        q = q_ref[pl.ds(r0, chunk), :]
        k = k_ref[pl.ds(r0, chunk), :]
        v = v_ref[pl.ds(r0, chunk), :]
        gcol = gc_ref[pl.ds(r0, chunk), :]
        grow = gr_ref[:, pl.ds(r0, chunk)]
        i_col = gcol[:, 2 * d:2 * d + 1]
        f_col = _log_sigmoid(gcol[:, 2 * d + 1:2 * d + 2])
        i_row = grow[2 * d:2 * d + 1, :]
        f_row = _log_sigmoid(grow[2 * d + 1:2 * d + 2, :])
        incl = (ss >= tt) if rev else (ss <= tt)
        incl_t = (tt >= ss) if rev else (tt <= ss)
        bcum_col = jnp.sum(jnp.where(incl, f_row, 0.0), axis=1, keepdims=True)
        bcum_row = jnp.sum(jnp.where(incl_t, f_col, 0.0), axis=0, keepdims=True)
        dmat = jnp.where(incl, bcum_col - bcum_row + i_row, -jnp.inf)
        m_prev = m_scr[d][:, 0:1]
        inter = bcum_col + m_prev
        mt = jnp.maximum(inter, jnp.max(dmat, axis=1, keepdims=True))
        w_inter = jnp.exp(inter - mt)
        s = _dot_nt(q, k) * (qscale * jnp.where(incl, jnp.exp(dmat - mt), 0.0))
        c_prev = c_scr[d]
        n_prev = n_scr[d]
        num = w_inter * (_dot(q, c_prev.astype(BF16)) * qscale) + _dot(s.astype(BF16), v)
        qn = jnp.sum(q.astype(F32) * n_prev, axis=1, keepdims=True) * qscale
        den = w_inter * qn + jnp.sum(s, axis=1, keepdims=True)
        h = num / jnp.maximum(jnp.abs(den), jnp.exp(-mt))
        last = 0 if rev else chunk - 1
        m_new = mt[last:last + 1, :]
        b_last = bcum_col[last:last + 1, :]
        decay = jnp.exp(b_last + m_prev - m_new)
        ws = jnp.exp(b_last - bcum_col + i_col - m_new)
        kw = k.astype(F32) * ws
        c_scr[d] = decay * c_prev + _dot(kw.T.astype(BF16), v)
        n_scr[d] = decay * n_prev + jnp.sum(kw, axis=0, keepdims=True)
        m_scr[d] = jnp.broadcast_to(m_new, (1, LANES))
        return r0, h

    def both(i, carry):
        r0, h = run_chunk(i, False)
        hsum_ref[pl.ds(r0, chunk), :] = h
        r1, hb = run_chunk(nc - 1 - i, True)
        hb_ref[pl.ds(r1, chunk), :] = hb
        return carry

    def gate_out(i, carry):
        r0 = pl.multiple_of(i * chunk, chunk)
        gate = _sigmoid(og_ref[pl.ds(r0, chunk), :].astype(F32))
        out_ref[pl.ds(r0, chunk), :] = (gate * (hsum_ref[pl.ds(r0, chunk), :]
                                                + hb_ref[pl.ds(r0, chunk), :])).astype(out_ref.dtype)
        return carry

    lax.fori_loop(0, nc, both, 0)
    lax.fori_loop(0, nc, gate_out, 0)

    if emit_state:
        co_ref[...] = c_scr[...]
        no_ref[...] = n_scr[...]
        mo_ref[...] = m_scr[...]


def _mlstm_core(p, gates, init, emit_state):
    b, t, _ = p.shape
    nh, dk, dv = MLSTM_HEADS, MLSTM_DK, MLSTM_DV
    chunk = min(MLSTM_CHUNK, t)
    g4 = gates.reshape(b, t, 4, nh)
    gcol = g4.transpose(0, 3, 1, 2)
    grow = g4.transpose(0, 3, 2, 1)
    args = [p, p, p, p, gcol, grow]
    in_specs = [
        pl.BlockSpec((None, t, dk), lambda i, h: (i, 0, h)),
        pl.BlockSpec((None, t, dk), lambda i, h: (i, 0, nh + h)),
        pl.BlockSpec((None, t, dv), lambda i, h: (i, 0, nh + h)),
        pl.BlockSpec((None, t, dv), lambda i, h: (i, 0, 2 * nh + h)),
        pl.BlockSpec((None, None, t, 4), lambda i, h: (i, h, 0, 0)),
        pl.BlockSpec((None, None, 4, t), lambda i, h: (i, h, 0, 0)),
    ]
    has_init = init is not None
    if has_init:
        c0, n0, m0 = init
        n0 = n0.transpose(0, 2, 1, 3).reshape(b, nh, 2, 1, dk)
        m0 = jnp.broadcast_to(m0.transpose(0, 2, 1)[..., None, None], (b, nh, 2, 1, LANES))
        args += [c0, n0, m0]
        in_specs += [
            pl.BlockSpec((None, 2, None, dk, dv), lambda i, h: (i, 0, h, 0, 0)),
            pl.BlockSpec((None, None, 2, 1, dk), lambda i, h: (i, h, 0, 0, 0)),
            pl.BlockSpec((None, None, 2, 1, LANES), lambda i, h: (i, h, 0, 0, 0)),
        ]
    out_shape = [jax.ShapeDtypeStruct((b, t, MLSTM_V), BF16)]
    out_specs = [pl.BlockSpec((None, t, dv), lambda i, h: (i, 0, h))]
    if emit_state:
        out_shape += [
            jax.ShapeDtypeStruct((b, 2, nh, dk, dv), F32),
            jax.ShapeDtypeStruct((b, nh, 2, 1, dk), F32),
            jax.ShapeDtypeStruct((b, nh, 2, 1, LANES), F32),
        ]
        out_specs += [
            pl.BlockSpec((None, 2, None, dk, dv), lambda i, h: (i, 0, h, 0, 0)),
            pl.BlockSpec((None, None, 2, 1, dk), lambda i, h: (i, h, 0, 0, 0)),
            pl.BlockSpec((None, None, 2, 1, LANES), lambda i, h: (i, h, 0, 0, 0)),
        ]
    kern = functools.partial(_mlstm_kernel, t_len=t, chunk=chunk, has_init=has_init,
                             emit_state=emit_state)
    outs = pl.pallas_call(
        kern,
        out_shape=out_shape,
        grid=(b, nh),
        in_specs=in_specs,
        out_specs=out_specs,
        scratch_shapes=[
            pltpu.VMEM((t, dv), F32),
            pltpu.VMEM((t, dv), F32),
            pltpu.VMEM((2, dk, dv), F32),
            pltpu.VMEM((2, 1, dk), F32),
            pltpu.VMEM((2, 1, LANES), F32),
        ],
        compiler_params=_cparams("parallel", "parallel"),
        name="mlstm",
    )(*args)
    if not emit_state:
        return outs[0], None
    hs, c_fin, n_fin, m_fin = outs
    n_fin = n_fin.reshape(b, nh, 2, dk).transpose(0, 2, 1, 3)
    m_fin = m_fin[:, :, :, 0, 0].transpose(0, 2, 1)
    return hs, (c_fin, n_fin, m_fin)


def _mlstm_mix(h, w_main, w_gate, b_gate, batch, init, emit_state):
    n, _ = h.shape
    t = n // batch
    p = _mm(h, w_main, BF16)
    gates = _mm(h, w_gate, F32, bias=b_gate)[:, :4 * MLSTM_HEADS]
    hs, state = _mlstm_core(p.reshape(batch, t, MLSTM_MAIN), gates.reshape(batch, t, 4 * MLSTM_HEADS),
                            init, emit_state)
    return hs.reshape(n, MLSTM_V), state


def _dft_tables(n, norm):
    idx = np.arange(n, dtype=np.int64)
    ang = 2.0 * np.pi * ((idx[:, None] * idx[None, :]) % n).astype(np.float64) / n
    return np.cos(ang) * norm, np.sin(ang) * norm


def _dft_cols_kernel(h_ref, cs_ref, o_ref):
    u = _dot(h_ref[...], cs_ref[...])
    o_ref[0] = u[:, :FOURIER_GW].astype(o_ref.dtype)
    o_ref[1] = u[:, FOURIER_GW:].astype(o_ref.dtype)


def _dft_cols(h, tm=512):
    n, d = h.shape
    gw = FOURIER_GW
    cos_c, sin_c = _dft_tables(gw, gw ** -0.5)
    cs = jnp.asarray(np.concatenate([cos_c, sin_c], axis=1), BF16)
    return pl.pallas_call(
        _dft_cols_kernel,
        out_shape=jax.ShapeDtypeStruct((2, n, d), BF16),
        grid=(n // tm, FOURIER_GROUPS),
        in_specs=[
            pl.BlockSpec((tm, gw), lambda i, g: (i, g)),
            pl.BlockSpec((gw, 2 * gw), lambda i, g: (0, 0)),
        ],
        out_specs=pl.BlockSpec((2, tm, gw), lambda i, g: (0, i, g)),
        compiler_params=_cparams("parallel", "arbitrary"),
        name="dft_cols",
    )(h, cs)


def _dft_rows_kernel(ct_ref, st_ref, uc_ref, us_ref, o_ref):
    o_ref[...] = (_dot(ct_ref[...], uc_ref[...]) + _dot(st_ref[...], us_ref[...])).astype(o_ref.dtype)


def _dft_rows(u, batch, tm=512, tn=512):
    _, n, d = u.shape
    t = n // batch
    tm = min(tm, t)
    cos_t, sin_t = _dft_tables(t, t ** -0.5)
    ct = jnp.asarray(cos_t, BF16)
    st = jnp.asarray(-sin_t, BF16)
    u4 = u.reshape(2, batch, t, d)
    out = pl.pallas_call(
        _dft_rows_kernel,
        out_shape=jax.ShapeDtypeStruct((batch, t, d), BF16),
        grid=(batch, d // tn, t // tm),
        in_specs=[
            pl.BlockSpec((tm, t), lambda b, j, i: (i, 0)),
            pl.BlockSpec((tm, t), lambda b, j, i: (i, 0)),
            pl.BlockSpec((None, None, t, tn), lambda b, j, i: (0, b, 0, j)),
            pl.BlockSpec((None, None, t, tn), lambda b, j, i: (1, b, 0, j)),
        ],
        out_specs=pl.BlockSpec((None, tm, tn), lambda b, j, i: (b, i, j)),
        compiler_params=_cparams("parallel", "parallel", "arbitrary"),
        name="dft_rows",
    )(ct, st, u4, u4)
    return out.reshape(n, d)


def _ctx_attn_kernel(q_ref, k_ref, v_ref, o_ref):
    q = q_ref[...].astype(BF16)
    k = k_ref[...].astype(BF16)
    v = v_ref[...].astype(BF16)
    s = _dot_nt(q, k) * (NA_HD ** -0.5)
    p = jnp.exp(s - jnp.max(s, axis=1, keepdims=True))
    l = jnp.sum(p, axis=1, keepdims=True)
    o_ref[...] = (_dot(p.astype(BF16), v) / l).astype(o_ref.dtype)


def _ctx_attention(qkv):
    b, s, _ = qkv.shape
    nh, dh = NA_HEADS, NA_HD
    return pl.pallas_call(
        _ctx_attn_kernel,
        out_shape=jax.ShapeDtypeStruct((b, s, D_MODEL), BF16),
        grid=(b, nh),
        in_specs=[
            pl.BlockSpec((None, s, dh), lambda i, h: (i, 0, h)),
            pl.BlockSpec((None, s, dh), lambda i, h: (i, 0, nh + h)),
            pl.BlockSpec((None, s, dh), lambda i, h: (i, 0, 2 * nh + h)),
        ],
        out_specs=pl.BlockSpec((None, s, dh), lambda i, h: (i, 0, h)),
        compiler_params=_cparams("parallel", "parallel"),
        name="ctx_attention",
    )(qkv, qkv, qkv)


def _na_bias_table(rel_bias, rows):
    w = GRID_W
    col = np.arange(w)
    col_start = np.clip(col - NA_WIN_COLS // 2, 0, w - NA_WIN_COLS)
    col_mask = (col[None, :] >= col_start[:, None]) & (col[None, :] < col_start[:, None] + NA_WIN_COLS)
    col_idx = np.clip(col[None, :] - col[:, None] + NA_WIN_COLS - 1, 0, 2 * NA_WIN_COLS - 2)
    nh = rel_bias.shape[0]
    bias_cols = jnp.where(col_mask[None, None], rel_bias[:, :, col_idx].astype(F32), NA_MASK)
    masked = jnp.full((nh, 1, w, w), NA_MASK, F32)
    planes = jnp.concatenate([bias_cols, masked], axis=1)
    n_blk = rows // NA_Q_ROWS
    plane_idx = np.empty((n_blk, NA_Q_ROWS, NA_K_ROWS), np.int32)
    for rb in range(n_blk):
        k0 = _na_key_row_start(rb, rows)
        for a in range(NA_Q_ROWS):
            r = rb * NA_Q_ROWS + a
            sr = min(max(r - NA_WIN_ROWS // 2, 0), rows - NA_WIN_ROWS)
            for kk in range(NA_K_ROWS):
                kr = k0 + kk
                inside = sr <= kr < sr + NA_WIN_ROWS
                plane_idx[rb, a, kk] = kr - r + NA_WIN_ROWS - 1 if inside else 2 * NA_WIN_ROWS - 1
    tab = planes[:, plane_idx.reshape(-1)]
    tab = tab.reshape(nh, n_blk, NA_Q_ROWS, NA_K_ROWS, w, w).transpose(0, 1, 2, 4, 3, 5)
    return tab.reshape(nh, n_blk, NA_Q_ROWS * w, NA_K_ROWS * w)


def _na_key_row_start(rb, rows):
    lo = rb * NA_Q_ROWS - NA_WIN_ROWS // 2
    return min(max(lo, 0), rows - NA_K_ROWS)


def _na_kernel(q_ref, k_ref, v_ref, kc_ref, vc_ref, b_ref, o_ref, *, rows):
    rb = pl.program_id(2)
    scale = NA_HD ** -0.5
    lo = rb * NA_Q_ROWS - NA_WIN_ROWS // 2
    start = pl.multiple_of(jnp.clip(lo, 0, rows - NA_K_ROWS) * GRID_W, GRID_W)
    nk = NA_K_ROWS * GRID_W
    q = q_ref[...]
    kw = k_ref[pl.ds(start, nk), :]
    vw = v_ref[pl.ds(start, nk), :]
    s_loc = _dot_nt(q, kw) * scale + b_ref[rb]
    s_ctx = _dot_nt(q, kc_ref[...]) * scale
    m = jnp.maximum(jnp.max(s_loc, axis=1, keepdims=True), jnp.max(s_ctx, axis=1, keepdims=True))
    p_loc = jnp.exp(s_loc - m)
    p_ctx = jnp.exp(s_ctx - m)
    l = jnp.sum(p_loc, axis=1, keepdims=True) + jnp.sum(p_ctx, axis=1, keepdims=True)
    o = _dot(p_loc.astype(BF16), vw) + _dot(p_ctx.astype(BF16), vc_ref[...])
    o_ref[...] = (o / l).astype(o_ref.dtype)


def _neighbourhood_attention(qkv, k_ctx, v_ctx, rel_bias):
    b, t, _ = qkv.shape
    nh, dh = NA_HEADS, NA_HD
    rows = t // GRID_W
    n_blk = rows // NA_Q_ROWS
    tq = NA_Q_ROWS * GRID_W
    tk = NA_K_ROWS * GRID_W
    past = k_ctx.shape[2]
    table = _na_bias_table(rel_bias, rows)
    kern = functools.partial(_na_kernel, rows=rows)
    return pl.pallas_call(
        kern,
        out_shape=jax.ShapeDtypeStruct((b, t, D_MODEL), BF16),
        grid=(nh, b, n_blk),
        in_specs=[
            pl.BlockSpec((None, tq, dh), lambda h, i, r: (i, r, h)),
            pl.BlockSpec((None, t, dh), lambda h, i, r: (i, 0, nh + h)),
            pl.BlockSpec((None, t, dh), lambda h, i, r: (i, 0, 2 * nh + h)),
            pl.BlockSpec((None, None, past, dh), lambda h, i, r: (i, h, 0, 0)),
            pl.BlockSpec((None, None, past, dh), lambda h, i, r: (i, h, 0, 0)),
            pl.BlockSpec((None, n_blk, tq, tk), lambda h, i, r: (h, 0, 0, 0)),
        ],
        out_specs=pl.BlockSpec((None, tq, dh), lambda h, i, r: (i, r, h)),
        compiler_params=_cparams("parallel", "parallel", "arbitrary"),
        name="neighbourhood_attention",
    )(qkv, qkv, qkv, k_ctx, v_ctx, table)


def kernel(x_prompt, x_sample, cache_na_k, cache_na_v, state_mlstm_C, state_mlstm_n, state_mlstm_m,
           c, c_ctx, w_mod, b_mod, norm_g, ffn_w_gate, ffn_w_up, ffn_w_down, pool_w, pool_scale,
           mlstm_w_in, mlstm_b_gates, mlstm_w_out, fourier_w_out, na_w_qkv, na_rel_bias, na_w_out):
    bc, sc, d = x_prompt.shape
    bl, sl, _ = x_sample.shape
    xc = x_prompt.reshape(bc * sc, d)
    xl = x_sample.reshape(bl * sl, d)

    cond = jnp.concatenate([c_ctx[None, :], c, jnp.zeros((MOD_ROWS - 1 - bl, d), F32)], axis=0)
    mod = _modulation(cond, w_mod, b_mod)

    wg = ffn_w_gate.astype(BF16)
    wu = ffn_w_up.astype(BF16)
    wd = ffn_w_down.astype(BF16)

    new_k = new_v = new_state = None
    for i in range(DEPTH):
        kind, j = i % 4, i // 4
        xc = _ffn(xc, mod, norm_g, wg, wu, wd, i, 0, None)
        xl = _ffn(xl, mod, norm_g, wg, wu, wd, i, 0, sl)
        if kind == 0:
            pw = pool_w[j].astype(BF16)
            hc = _prenorm(xc, mod, norm_g, i, None, F32)
            hl = _prenorm(xl, mod, norm_g, i, sl, F32)
            yc = _pool_mix(hc.reshape(bc, sc, d), pw, pool_scale[j]).reshape(bc * sc, d)
            yl = _pool_mix(hl.reshape(bl, sl, d), pw, pool_scale[j]).reshape(bl * sl, d)
            w_out = None
        elif kind == 1:
            hc = _prenorm(xc, mod, norm_g, i, None, BF16)
            hl = _prenorm(xl, mod, norm_g, i, sl, BF16)
            w_in = mlstm_w_in[j]
            w_main = w_in[:, :MLSTM_MAIN].astype(BF16)
            n_gate = 4 * MLSTM_HEADS
            w_gate = jnp.pad(w_in[:, MLSTM_MAIN:], ((0, 0), (0, LANES - n_gate))).astype(BF16)
            b_gate = jnp.pad(mlstm_b_gates[j].astype(F32), (0, LANES - n_gate))
            yc, new_state = _mlstm_mix(hc, w_main, w_gate, b_gate, bc, None, True)
            init = (state_mlstm_C[:, j].astype(F32), state_mlstm_n[:, j].astype(F32),
                    state_mlstm_m[:, j].astype(F32))
            yl, _ = _mlstm_mix(hl, w_main, w_gate, b_gate, bl, init, False)
            w_out = mlstm_w_out[j].astype(BF16)
        elif kind == 2:
            hc = _prenorm(xc, mod, norm_g, i, None, BF16)
            hl = _prenorm(xl, mod, norm_g, i, sl, BF16)
            yc = _dft_rows(_dft_cols(hc), bc)
            yl = _dft_rows(_dft_cols(hl), bl)
            w_out = fourier_w_out[j].astype(BF16)
        else:
            hc = _prenorm(xc, mod, norm_g, i, None, BF16)
            hl = _prenorm(xl, mod, norm_g, i, sl, BF16)
            w_qkv = na_w_qkv[j].astype(BF16)
            qkv_c = _mm(hc, w_qkv, F32).reshape(bc, sc, 3 * d)
            new_k = qkv_c[:, :, d:2 * d].reshape(bc, 1, sc, NA_HEADS, NA_HD)
            new_v = qkv_c[:, :, 2 * d:].reshape(bc, 1, sc, NA_HEADS, NA_HD)
            yc = _ctx_attention(qkv_c).reshape(bc * sc, d)
            qkv_l = _mm(hl, w_qkv, BF16).reshape(bl, sl, 3 * d)
            k_ctx = cache_na_k[:, j].transpose(0, 2, 1, 3).astype(BF16)
            v_ctx = cache_na_v[:, j].transpose(0, 2, 1, 3).astype(BF16)
            yl = _neighbourhood_attention(qkv_l, k_ctx, v_ctx, na_rel_bias[j]).reshape(bl * sl, d)
            w_out = na_w_out[j].astype(BF16)
        xc = _mixer_residual(yc, w_out, xc, mod, norm_g, i, None)
        xl = _mixer_residual(yl, w_out, xl, mod, norm_g, i, sl)
        xc = _ffn(xc, mod, norm_g, wg, wu, wd, i, 1, None)
        xl = _ffn(xl, mod, norm_g, wg, wu, wd, i, 1, sl)

    c_fin, n_fin, m_fin = new_state
    return (xc.reshape(bc, sc, d), xl.reshape(bl, sl, d), new_k, new_v,
            c_fin[:, None], n_fin[:, None], m_fin[:, None])
```
